```python
import numpy as np
import jax, jax.numpy as jnp
from jax import lax

D_MODEL = 2048
BATCH = 4
SEQ = 8192
DEPTH = 4

CTX_LEN = 256
GRID_W = 64
MIX_WIDTH = D_MODEL
NA_WIDTH = D_MODEL // 2
NA_HEADS = 8
NA_HEAD_DIM = NA_WIDTH // NA_HEADS
CONV_WIDTH = MIX_WIDTH - NA_WIDTH
CONV_K = 3
WIN_R = 8
WIN_C = 16
D_FF = 4 * D_MODEL
N_MOD = 6
IN_SPLITS = tuple(int(s) for s in np.cumsum([NA_WIDTH, NA_WIDTH, NA_WIDTH, CONV_WIDTH, CONV_WIDTH]))
IN_COLS = 3 * NA_WIDTH + 3 * CONV_WIDTH
EPS = 1e-6

kernel_name = "hymba_natten_shortconv_dit_block"


def rms_norm(x, g):
    xf = x.astype(jnp.float32)
    y = xf * lax.rsqrt(jnp.mean(xf * xf, axis=-1, keepdims=True) + EPS)
    return (y * g.astype(jnp.float32)).astype(x.dtype)


def modulate(h, shift, scale):
    return h * (1 + scale) + shift


def to_heads(t):
    b, l, _ = t.shape
    return t.reshape(b, l, NA_HEADS, NA_HEAD_DIM).transpose(0, 2, 1, 3)


def from_heads(t):
    b, h, l, d = t.shape
    return t.transpose(0, 2, 1, 3).reshape(b, l, h * d)


def short_conv(u, w):
    l = u.shape[1]
    half = CONV_K // 2
    up = jnp.pad(u, ((0, 0), (half, half), (0, 0)))
    return sum(up[:, k:k + l] * w[k] for k in range(CONV_K))


def conv_mixer(b_gate, c_gate, v, w):
    return b_gate * short_conv(c_gate * v, w)


def context_attention(q, k, v):
    s = jnp.einsum('bhqd,bhkd->bhqk', q, k).astype(jnp.float32) * (NA_HEAD_DIM ** -0.5)
    p = jax.nn.softmax(s, axis=-1).astype(v.dtype)
    return jnp.einsum('bhqk,bhkd->bhqd', p, v)


def neighbourhood_attention(q, k, v, k_ctx, v_ctx, rpb):
    b, h, s, dh = q.shape
    rows = s // GRID_W
    kr = min(WIN_R, rows)
    kc = WIN_C
    kg = k.reshape(b, h, rows, GRID_W, dh)
    vg = v.reshape(b, h, rows, GRID_W, dh)
    q_rows = q.reshape(b, h, rows, GRID_W, dh).transpose(2, 0, 1, 3, 4)
    col = np.arange(GRID_W)
    col_start = np.clip(col - kc // 2, 0, GRID_W - kc)
    col_idx = col_start[:, None] + np.arange(kc)[None, :]
    dc = col_idx - col[:, None] + (WIN_C - 1)
    scale = NA_HEAD_DIM ** -0.5
    n_loc = kr * kc

    def one_row(args):
        r, q_r = args
        rs = jnp.clip(r - kr // 2, 0, rows - kr)
        k_rows = lax.dynamic_slice_in_dim(kg, rs, kr, axis=2)
        v_rows = lax.dynamic_slice_in_dim(vg, rs, kr, axis=2)
        k_win = k_rows[:, :, :, col_idx]
        v_win = v_rows[:, :, :, col_idx]
        s_loc = jnp.einsum('bhqd,bhaqcd->bhqac', q_r, k_win).astype(jnp.float32) * scale
        dr = rs + jnp.arange(kr) - r + (WIN_R - 1)
        bias = rpb[:, dr[:, None, None], dc[None, :, :]]
        s_loc = s_loc + bias.transpose(0, 2, 1, 3)[None].astype(jnp.float32)
        s_ctx = jnp.einsum('bhqd,bhkd->bhqk', q_r, k_ctx).astype(jnp.float32) * scale
        sc = jnp.concatenate([s_loc.reshape(b, h, GRID_W, n_loc), s_ctx], axis=-1)
        p = jax.nn.softmax(sc, axis=-1).astype(v.dtype)
        p_loc = p[..., :n_loc].reshape(b, h, GRID_W, kr, kc)
        p_ctx = p[..., n_loc:]
        return (jnp.einsum('bhqac,bhaqcd->bhqd', p_loc, v_win)
                + jnp.einsum('bhqk,bhkd->bhqd', p_ctx, v_ctx))

    out = lax.map(one_row, (jnp.arange(rows), q_rows))
    return out.transpose(1, 2, 0, 3, 4).reshape(b, h, s, dh)


def merge_and_project(o_na, o_conv, g_na, g_conv, w_o):
    mixed = jnp.concatenate([rms_norm(o_na, g_na), rms_norm(o_conv, g_conv)], axis=-1)
    return mixed @ w_o


def sq_relu_mlp(h, w1, w2):
    return jnp.square(jax.nn.relu(h @ w1)) @ w2


def setup_inputs(seed: int = 0) -> dict:
    key = jax.random.key(seed)
    ks = jax.random.split(key, 17)
    f32 = jnp.float32
    nrm = lambda k, shape, s: jax.random.normal(k, shape, f32) * s
    return {
        "x": nrm(ks[0], (BATCH, SEQ, D_MODEL), 1.0),
        "c": nrm(ks[1], (BATCH, D_MODEL), 1.0),
        "ctx": nrm(ks[2], (BATCH, CTX_LEN, D_MODEL), 1.0),
        "c_ctx": nrm(ks[3], (D_MODEL,), 1.0),
        "w_ada": nrm(ks[4], (DEPTH, D_MODEL, N_MOD * D_MODEL), D_MODEL ** -0.5),
        "b_ada": nrm(ks[5], (DEPTH, N_MOD * D_MODEL), 0.02),
        "g_norm1": 1.0 + nrm(ks[6], (DEPTH, D_MODEL), 0.05),
        "w_in": nrm(ks[7], (DEPTH, D_MODEL, IN_COLS), D_MODEL ** -0.5),
        "rpb": nrm(ks[8], (DEPTH, NA_HEADS, 2 * WIN_R - 1, 2 * WIN_C - 1), 0.1),
        "conv_w": nrm(ks[9], (DEPTH, CONV_K, CONV_WIDTH), CONV_K ** -0.5),
        "g_na_out": 1.0 + nrm(ks[10], (DEPTH, NA_WIDTH), 0.05),
        "g_conv_out": 1.0 + nrm(ks[11], (DEPTH, CONV_WIDTH), 0.05),
        "w_out": nrm(ks[12], (DEPTH, MIX_WIDTH, D_MODEL), MIX_WIDTH ** -0.5),
        "g_norm2": 1.0 + nrm(ks[13], (DEPTH, D_MODEL), 0.05),
        "w_mlp1": nrm(ks[14], (DEPTH, D_MODEL, D_FF), D_MODEL ** -0.5),
        "w_mlp2": nrm(ks[15], (DEPTH, D_FF, D_MODEL), D_FF ** -0.5),
        "g_final": 1.0 + nrm(ks[16], (D_MODEL,), 0.05),
    }


def reference(x, c, ctx, c_ctx, w_ada, b_ada, g_norm1, w_in, rpb, conv_w, g_na_out, g_conv_out,
              w_out, g_norm2, w_mlp1, w_mlp2, g_final):
    s_lat = jax.nn.silu(c)
    s_ctx = jax.nn.silu(c_ctx)
    for i in range(DEPTH):
        last = i == DEPTH - 1
        m_lat = (s_lat @ w_ada[i] + b_ada[i])[:, None, :]
        sh1, sc1, gt1, sh2, sc2, gt2 = jnp.split(m_lat, N_MOD, axis=-1)
        m_ctx = s_ctx @ w_ada[i] + b_ada[i]
        csh1, csc1, cgt1, csh2, csc2, cgt2 = jnp.split(m_ctx, N_MOD, axis=-1)

        h_lat = modulate(rms_norm(x, g_norm1[i]), sh1, sc1)
        h_ctx = modulate(rms_norm(ctx, g_norm1[i]), csh1, csc1)
        q, k, v, bg, cg, u = jnp.split(h_lat @ w_in[i], IN_SPLITS, axis=-1)
        qc, kc, vc, bgc, cgc, uc = jnp.split(h_ctx @ w_in[i], IN_SPLITS, axis=-1)
        k_ctx_h, v_ctx_h = to_heads(kc), to_heads(vc)

        o_na = from_heads(neighbourhood_attention(to_heads(q), to_heads(k), to_heads(v),
                                                  k_ctx_h, v_ctx_h, rpb[i]))
        o_conv = conv_mixer(bg, cg, u, conv_w[i])
        x_new = x + gt1 * merge_and_project(o_na, o_conv, g_na_out[i], g_conv_out[i], w_out[i])

        if not last:
            oc_na = from_heads(context_attention(to_heads(qc), k_ctx_h, v_ctx_h))
            oc_conv = conv_mixer(bgc, cgc, uc, conv_w[i])
            ctx = ctx + cgt1 * merge_and_project(oc_na, oc_conv, g_na_out[i], g_conv_out[i], w_out[i])
            hc2 = modulate(rms_norm(ctx, g_norm2[i]), csh2, csc2)
            ctx = ctx + cgt2 * sq_relu_mlp(hc2, w_mlp1[i], w_mlp2[i])

        h2 = modulate(rms_norm(x_new, g_norm2[i]), sh2, sc2)
        x = x_new + gt2 * sq_relu_mlp(h2, w_mlp1[i], w_mlp2[i])
    return rms_norm(x, g_final)
```

```python
import functools

import numpy as np
import jax
import jax.numpy as jnp
from jax import lax
from jax.experimental import pallas as pl
from jax.experimental.pallas import tpu as pltpu

D_MODEL = 2048
BATCH = 4
SEQ = 8192
DEPTH = 4
CTX_LEN = 256
GRID_W = 64
GRID_ROWS = SEQ // GRID_W
NA_WIDTH = D_MODEL // 2
NA_HEADS = 8
HEAD_DIM = NA_WIDTH // NA_HEADS
CONV_WIDTH = D_MODEL - NA_WIDTH
CONV_K = 3
WIN_R = 8
WIN_C = 16
D_FF = 4 * D_MODEL
N_MOD = 6
IN_COLS = 3 * NA_WIDTH + 3 * CONV_WIDTH
EPS = 1e-6

N_LAT = BATCH * SEQ
N_CTX = BATCH * CTX_LEN
N_TOK = N_LAT + N_CTX

V7X_LANES = 128
V7X_BF16_SUBLANES = 16
V7X_VMEM_BYTES = 64 * 1024 * 1024

TILE_M = 1024
N_TILES = N_TOK // TILE_M
N_LAT_TILES = N_LAT // TILE_M
TILE_M_OUT = 512
ROW_CHUNK = 128
TILE_N_IN = 1024
TILE_FF = 512
TILE_N_ADA = 1024
MOD_ROWS = 8

ATT_R = 4
ATT_KR = ATT_R + WIN_R - 1
ATT_MQ = ATT_R * GRID_W
ATT_NK = ATT_KR * GRID_W
ATT_NRB = GRID_ROWS // ATT_R
ATT_RB = 8
ATT_TYPES = 3
NEG_BIG = -1e30

_F32 = jnp.float32
_BF16 = jnp.bfloat16


def _vmem_limit(nbytes):
    return int(min(nbytes, V7X_VMEM_BYTES - 4 * 1024 * 1024))


def _mod_row(i, tile_m=TILE_M):
    return jnp.minimum(i // (SEQ // tile_m), BATCH)


def _ada_kernel(c_ref, w_ref, b_ref, o_ref):
    c = c_ref[...]
    s = c / (1.0 + jnp.exp(-c))
    o_ref[0] = jnp.dot(s.astype(_BF16), w_ref[0].astype(_BF16),
                       preferred_element_type=_F32) + b_ref[0]


def _ada_modulation(cc, w_ada, b_ada):
    n = N_MOD * D_MODEL
    return pl.pallas_call(
        _ada_kernel,
        out_shape=jax.ShapeDtypeStruct((DEPTH, MOD_ROWS, n), _F32),
        grid=(DEPTH, n // TILE_N_ADA),
        in_specs=[
            pl.BlockSpec((MOD_ROWS, D_MODEL), lambda i, j: (0, 0)),
            pl.BlockSpec((1, D_MODEL, TILE_N_ADA), lambda i, j: (i, 0, j)),
            pl.BlockSpec((1, 1, TILE_N_ADA), lambda i, j: (i, 0, j)),
        ],
        out_specs=pl.BlockSpec((1, MOD_ROWS, TILE_N_ADA), lambda i, j: (i, 0, j)),
        compiler_params=pltpu.CompilerParams(
            dimension_semantics=("arbitrary", "arbitrary"),
            vmem_limit_bytes=_vmem_limit(40 * 1024 * 1024)),
        name="ada_modulation",
    )(cc, w_ada, b_ada.reshape(DEPTH, 1, n))


def _bias_block_plan():
    plan = np.full((ATT_TYPES, ATT_R, ATT_KR), -1, np.int64)
    for t, rb in enumerate((0, 1, ATT_NRB - 1)):
        r0 = rb * ATT_R
        ks = int(np.clip(r0 - WIN_R // 2, 0, GRID_ROWS - ATT_KR))
        for ri in range(ATT_R):
            r = r0 + ri
            rs = int(np.clip(r - WIN_R // 2, 0, GRID_ROWS - WIN_R))
            for a in range(ATT_KR):
                kr = ks + a
                if rs <= kr < rs + WIN_R:
                    plan[t, ri, a] = kr - r + (WIN_R - 1)
    return plan


def _bias_kernel(rpb_ref, o_ref, m_ref, *, plan):
    ih = pl.program_id(0)
    n_dr, n_dc = 2 * WIN_R - 1, 2 * WIN_C - 1
    base = ih * (n_dr * n_dc)
    cq = lax.broadcasted_iota(jnp.int32, (GRID_W, GRID_W), 0)
    ck = lax.broadcasted_iota(jnp.int32, (GRID_W, GRID_W), 1)
    cs = jnp.clip(cq - WIN_C // 2, 0, GRID_W - WIN_C)
    in_win = (ck >= cs) & (ck < cs + WIN_C)
    dsel = jnp.where(in_win, ck - cq + (WIN_C - 1), -1)
    neg = jnp.full((GRID_W, GRID_W), NEG_BIG, _F32)
    for dr in range(n_dr):
        m = neg
        for d in range(n_dc):
            m = jnp.where(dsel == d, rpb_ref[base + dr * n_dc + d], m)
        m_ref[dr] = m
    for t in range(ATT_TYPES):
        for ri in range(ATT_R):
            for a in range(ATT_KR):
                dr = int(plan[t, ri, a])
                blk = m_ref[dr] if dr >= 0 else neg
                o_ref[0, t, ri * GRID_W:(ri + 1) * GRID_W, a * GRID_W:(a + 1) * GRID_W] = blk


def _bias_tables(rpb):
    n_dr, n_dc = 2 * WIN_R - 1, 2 * WIN_C - 1
    return pl.pallas_call(
        functools.partial(_bias_kernel, plan=_bias_block_plan()),
        out_shape=jax.ShapeDtypeStruct((DEPTH * NA_HEADS, ATT_TYPES, ATT_MQ, ATT_NK), _F32),
        grid=(DEPTH * NA_HEADS,),
        in_specs=[pl.BlockSpec(memory_space=pltpu.SMEM)],
        out_specs=pl.BlockSpec((1, ATT_TYPES, ATT_MQ, ATT_NK), lambda i: (i, 0, 0, 0)),
        scratch_shapes=[pltpu.VMEM((n_dr, GRID_W, GRID_W), _F32)],
        compiler_params=pltpu.CompilerParams(dimension_semantics=("arbitrary",)),
        name="attn_bias_tables",
    )(rpb.reshape(DEPTH * NA_HEADS * n_dr * n_dc))


def _norm_modulate_tile(x_ref, h_ref, gain, shift):
    def body(c, carry):
        r0 = pl.multiple_of(c * ROW_CHUNK, ROW_CHUNK)
        x = x_ref[pl.ds(r0, ROW_CHUNK), :]
        ms = jnp.mean(x * x, axis=-1, keepdims=True)
        h = x * lax.rsqrt(ms + EPS) * gain + shift
        h_ref[pl.ds(r0, ROW_CHUNK), :] = h.astype(_BF16)
        return carry
    lax.fori_loop(0, TILE_M // ROW_CHUNK, body, 0)


def _inproj_kernel(x_ref, sh_ref, sc_ref, g_ref, w_ref, o_ref, h_ref):
    j = pl.program_id(1)

    @pl.when(j == 0)
    def _():
        _norm_modulate_tile(x_ref, h_ref, g_ref[...] * (1.0 + sc_ref[0]), sh_ref[0])

    acc = jnp.dot(h_ref[...], w_ref[...], preferred_element_type=_F32)
    scale = jnp.where(j < NA_WIDTH // TILE_N_IN, HEAD_DIM ** -0.5, 1.0).astype(_F32)
    o_ref[...] = (acc * scale).astype(_BF16)


def _inproj(x, mod, g, w_bf16):
    return pl.pallas_call(
        _inproj_kernel,
        out_shape=jax.ShapeDtypeStruct((N_TOK, IN_COLS), _BF16),
        grid=(N_TILES, IN_COLS // TILE_N_IN),
        in_specs=[
            pl.BlockSpec((TILE_M, D_MODEL), lambda i, j: (i, 0)),
            pl.BlockSpec((1, 1, D_MODEL), lambda i, j: (_mod_row(i), 0, 0)),
            pl.BlockSpec((1, 1, D_MODEL), lambda i, j: (_mod_row(i), 0, 1)),
            pl.BlockSpec((1, D_MODEL), lambda i, j: (0, 0)),
            pl.BlockSpec((D_MODEL, TILE_N_IN), lambda i, j: (0, j)),
        ],
        out_specs=pl.BlockSpec((TILE_M, TILE_N_IN), lambda i, j: (i, j)),
        scratch_shapes=[pltpu.VMEM((TILE_M, D_MODEL), _BF16)],
        compiler_params=pltpu.CompilerParams(
            dimension_semantics=("arbitrary", "arbitrary"),
            vmem_limit_bytes=_vmem_limit(48 * 1024 * 1024)),
        name="inproj",
    )(x, mod, mod, g.reshape(1, D_MODEL), w_bf16)


_NT_DIMS = (((1,), (1,)), ((), ()))


def _attn_kernel(q_ref, k_ref, v_ref, kc_ref, vc_ref, bias_ref, o_ref):
    g = pl.program_id(2)
    kc = kc_ref[...]
    vc = vc_ref[...]

    def body(t, carry):
        rb = g * ATT_RB + t
        ks = jnp.clip(rb * ATT_R - WIN_R // 2, 0, GRID_ROWS - ATT_KR)
        typ = jnp.where(rb == 0, 0, jnp.where(rb == ATT_NRB - 1, 2, 1))
        q0 = pl.multiple_of(t * ATT_MQ, ATT_MQ)
        k0 = pl.multiple_of(ks * GRID_W, GRID_W)
        q = q_ref[pl.ds(q0, ATT_MQ), :]
        kw = k_ref[pl.ds(k0, ATT_NK), :]
        vw = v_ref[pl.ds(k0, ATT_NK), :]
        s_loc = lax.dot_general(q, kw, _NT_DIMS, preferred_element_type=_F32) + bias_ref[0, typ]
        s_ctx = lax.dot_general(q, kc, _NT_DIMS, preferred_element_type=_F32)
        m = jnp.maximum(jnp.max(s_loc, axis=-1, keepdims=True),
                        jnp.max(s_ctx, axis=-1, keepdims=True))
        e_loc = jnp.exp(s_loc - m)
        e_ctx = jnp.exp(s_ctx - m)
        den = jnp.sum(e_loc, axis=-1, keepdims=True) + jnp.sum(e_ctx, axis=-1, keepdims=True)
        o = (jnp.dot(e_loc.astype(_BF16), vw, preferred_element_type=_F32)
             + jnp.dot(e_ctx.astype(_BF16), vc, preferred_element_type=_F32))
        o_ref[pl.ds(q0, ATT_MQ), :] = (o / den).astype(_BF16)
        return carry

    lax.fori_loop(0, ATT_RB, body, 0)


def _attention(p, bias):
    steps = ATT_NRB // ATT_RB
    qrows = ATT_RB * ATT_MQ
    k_col = NA_WIDTH // HEAD_DIM
    ctx_blk0 = N_LAT // CTX_LEN
    return pl.pallas_call(
        _attn_kernel,
        out_shape=jax.ShapeDtypeStruct((N_TOK, NA_WIDTH), _BF16),
        grid=(BATCH, NA_HEADS, steps),
        in_specs=[
            pl.BlockSpec((qrows, HEAD_DIM), lambda b, h, g: (b * steps + g, h)),
            pl.BlockSpec((SEQ, HEAD_DIM), lambda b, h, g: (b, k_col + h)),
            pl.BlockSpec((SEQ, HEAD_DIM), lambda b, h, g: (b, 2 * k_col + h)),
            pl.BlockSpec((CTX_LEN, HEAD_DIM), lambda b, h, g: (ctx_blk0 + b, k_col + h)),
            pl.BlockSpec((CTX_LEN, HEAD_DIM), lambda b, h, g: (ctx_blk0 + b, 2 * k_col + h)),
            pl.BlockSpec((1, ATT_TYPES, ATT_MQ, ATT_NK), lambda b, h, g: (h, 0, 0, 0)),
        ],
        out_specs=pl.BlockSpec((qrows, HEAD_DIM), lambda b, h, g: (b * steps + g, h)),
        compiler_params=pltpu.CompilerParams(
            dimension_semantics=("arbitrary", "arbitrary", "arbitrary"),
            vmem_limit_bytes=_vmem_limit(40 * 1024 * 1024)),
        name="nbr_attention",
    )(p, p, p, p, p, bias)


def _ctx_attn_kernel(q_ref, k_ref, v_ref, o_in_ref, o_ref):
    del o_in_ref
    q = q_ref[...]
    s = lax.dot_general(q, k_ref[...], _NT_DIMS, preferred_element_type=_F32)
    m = jnp.max(s, axis=-1, keepdims=True)
    e = jnp.exp(s - m)
    den = jnp.sum(e, axis=-1, keepdims=True)
    o = jnp.dot(e.astype(_BF16), v_ref[...], preferred_element_type=_F32)
    o_ref[...] = (o / den).astype(_BF16)


def _ctx_attention(p, o_na):
    k_col = NA_WIDTH // HEAD_DIM
    ctx_blk0 = N_LAT // CTX_LEN
    return pl.pallas_call(
        _ctx_attn_kernel,
        out_shape=jax.ShapeDtypeStruct((N_TOK, NA_WIDTH), _BF16),
        grid=(BATCH, NA_HEADS),
        in_specs=[
            pl.BlockSpec((CTX_LEN, HEAD_DIM), lambda b, h: (ctx_blk0 + b, h)),
            pl.BlockSpec((CTX_LEN, HEAD_DIM), lambda b, h: (ctx_blk0 + b, k_col + h)),
            pl.BlockSpec((CTX_LEN, HEAD_DIM), lambda b, h: (ctx_blk0 + b, 2 * k_col + h)),
            pl.BlockSpec(memory_space=pl.ANY),
        ],
        out_specs=pl.BlockSpec((CTX_LEN, HEAD_DIM), lambda b, h: (ctx_blk0 + b, h)),
        input_output_aliases={3: 0},
        compiler_params=pltpu.CompilerParams(dimension_semantics=("arbitrary", "arbitrary")),
        name="ctx_attention",
    )(p, p, p, o_na)


_HALO = V7X_BF16_SUBLANES
_WPAD = 8


def _outproj_kernel(ona_ref, bg_ref, cg_ref, u_ref, cgp_ref, up_ref, cgn_ref, un_ref,
                    x_ref, gt_ref, gna_ref, gcv_ref, cw_ref, w_ref, o_ref, wbuf_ref, mix_ref):
    i = pl.program_id(0)
    tm = TILE_M_OUT
    seq_len = jnp.where(i < N_LAT // tm, SEQ, CTX_LEN)

    wbuf_ref[_WPAD:_WPAD + tm, :] = cg_ref[...].astype(_F32) * u_ref[...].astype(_F32)
    wbuf_ref[_WPAD - 1:_WPAD, :] = (cgp_ref[_HALO - 1:_HALO, :].astype(_F32)
                                    * up_ref[_HALO - 1:_HALO, :].astype(_F32))
    wbuf_ref[_WPAD + tm:_WPAD + tm + 1, :] = (cgn_ref[0:1, :].astype(_F32)
                                              * un_ref[0:1, :].astype(_F32))
    cw = cw_ref[...]
    gna = gna_ref[...]
    gcv = gcv_ref[...]

    for c in range(tm // ROW_CHUNK):
        r0 = c * ROW_CHUNK
        row = lax.broadcasted_iota(jnp.int32, (ROW_CHUNK, 1), 0) + (i * tm + r0)
        pos = row & (seq_len - 1)
        w_prev = jnp.where(pos == 0, 0.0, wbuf_ref[_WPAD - 1 + r0:_WPAD - 1 + r0 + ROW_CHUNK, :])
        w_mid = wbuf_ref[_WPAD + r0:_WPAD + r0 + ROW_CHUNK, :]
        w_next = jnp.where(pos == seq_len - 1, 0.0,
                           wbuf_ref[_WPAD + 1 + r0:_WPAD + 1 + r0 + ROW_CHUNK, :])
        conv = w_prev * cw[0:1, :] + w_mid * cw[1:2, :] + w_next * cw[2:3, :]
        oc = bg_ref[r0:r0 + ROW_CHUNK, :].astype(_F32) * conv
        oc_n = oc * lax.rsqrt(jnp.mean(oc * oc, axis=-1, keepdims=True) + EPS) * gcv
        on = ona_ref[r0:r0 + ROW_CHUNK, :].astype(_F32)
        on_n = on * lax.rsqrt(jnp.mean(on * on, axis=-1, keepdims=True) + EPS) * gna
        mix_ref[r0:r0 + ROW_CHUNK, 0:NA_WIDTH] = on_n.astype(_BF16)
        mix_ref[r0:r0 + ROW_CHUNK, NA_WIDTH:D_MODEL] = oc_n.astype(_BF16)

    y = jnp.dot(mix_ref[...], w_ref[...], preferred_element_type=_F32)
    o_ref[...] = x_ref[...] + gt_ref[0] * y


def _outproj(o_na, p, x, mod, g_na, g_conv, conv_w, w_bf16, with_ctx):
    tm = TILE_M_OUT
    n_tiles = (N_TOK if with_ctx else N_LAT) // tm
    cb = NA_WIDTH // CONV_WIDTH * 3
    halo_per_tile = tm // _HALO
    last_halo = N_TOK // _HALO - 1
    prev_map = lambda col: (lambda i: (jnp.maximum(i * halo_per_tile - 1, 0), col))
    next_map = lambda col: (lambda i: (jnp.minimum((i + 1) * halo_per_tile, last_halo), col))
    return pl.pallas_call(
        _outproj_kernel,
        out_shape=jax.ShapeDtypeStruct((N_TOK, D_MODEL), _F32),
        grid=(n_tiles,),
        in_specs=[
            pl.BlockSpec((tm, NA_WIDTH), lambda i: (i, 0)),
            pl.BlockSpec((tm, CONV_WIDTH), lambda i: (i, cb)),
            pl.BlockSpec((tm, CONV_WIDTH), lambda i: (i, cb + 1)),
            pl.BlockSpec((tm, CONV_WIDTH), lambda i: (i, cb + 2)),
            pl.BlockSpec((_HALO, CONV_WIDTH), prev_map(cb + 1)),
            pl.BlockSpec((_HALO, CONV_WIDTH), prev_map(cb + 2)),
            pl.BlockSpec((_HALO, CONV_WIDTH), next_map(cb + 1)),
            pl.BlockSpec((_HALO, CONV_WIDTH), next_map(cb + 2)),
            pl.BlockSpec((tm, D_MODEL), lambda i: (i, 0)),
            pl.BlockSpec((1, 1, D_MODEL), lambda i: (_mod_row(i, tm), 0, 2)),
            pl.BlockSpec((1, NA_WIDTH), lambda i: (0, 0)),
            pl.BlockSpec((1, CONV_WIDTH), lambda i: (0, 0)),
            pl.BlockSpec((CONV_K, CONV_WIDTH), lambda i: (0, 0)),
            pl.BlockSpec((D_MODEL, D_MODEL), lambda i: (0, 0)),
        ],
        out_specs=pl.BlockSpec((tm, D_MODEL), lambda i: (i, 0)),
        scratch_shapes=[pltpu.VMEM((tm + 2 * _WPAD, CONV_WIDTH), _F32),
                        pltpu.VMEM((tm, D_MODEL), _BF16)],
        input_output_aliases={8: 0},
        compiler_params=pltpu.CompilerParams(
            dimension_semantics=("arbitrary",),
            vmem_limit_bytes=_vmem_limit(60 * 1024 * 1024)),
        name="outproj",
    )(o_na, p, p, p, p, p, p, p, x, mod, g_na.reshape(1, NA_WIDTH),
      g_conv.reshape(1, CONV_WIDTH), conv_w, w_bf16)


def _mlp_kernel(x_ref, sh_ref, sc_ref, gt_ref, g_ref, w1_ref, w2_ref, o_ref, h_ref):
    f = pl.program_id(1)

    @pl.when(f == 0)
    def _():
        _norm_modulate_tile(x_ref, h_ref, g_ref[...] * (1.0 + sc_ref[0]), sh_ref[0])
        o_ref[...] = jnp.zeros_like(o_ref)

    a = jnp.dot(h_ref[...], w1_ref[...], preferred_element_type=_F32)
    a = jnp.maximum(a, 0.0)
    a = a * a
    o_ref[...] += jnp.dot(a.astype(_BF16), w2_ref[...], preferred_element_type=_F32)

    @pl.when(f == pl.num_programs(1) - 1)
    def _():
        o_ref[...] = x_ref[...] + gt_ref[0] * o_ref[...]


def _mlp(x, mod, g, w1_bf16, w2_bf16, with_ctx):
    n_tiles = N_TILES if with_ctx else N_LAT_TILES
    return pl.pallas_call(
        _mlp_kernel,
        out_shape=jax.ShapeDtypeStruct((N_TOK, D_MODEL), _F32),
        grid=(n_tiles, D_FF // TILE_FF),
        in_specs=[
            pl.BlockSpec((TILE_M, D_MODEL), lambda i, f: (i, 0)),
            pl.BlockSpec((1, 1, D_MODEL), lambda i, f: (_mod_row(i), 0, 3)),
            pl.BlockSpec((1, 1, D_MODEL), lambda i, f: (_mod_row(i), 0, 4)),
            pl.BlockSpec((1, 1, D_MODEL), lambda i, f: (_mod_row(i), 0, 5)),
            pl.BlockSpec((1, D_MODEL), lambda i, f: (0, 0)),
            pl.BlockSpec((D_MODEL, TILE_FF), lambda i, f: (0, f)),
            pl.BlockSpec((TILE_FF, D_MODEL), lambda i, f: (f, 0)),
        ],
        out_specs=pl.BlockSpec((TILE_M, D_MODEL), lambda i, f: (i, 0)),
        scratch_shapes=[pltpu.VMEM((TILE_M, D_MODEL), _BF16)],
        input_output_aliases={0: 0},
        compiler_params=pltpu.CompilerParams(
            dimension_semantics=("arbitrary", "arbitrary"),
            vmem_limit_bytes=_vmem_limit(60 * 1024 * 1024)),
        name="mlp",
    )(x, mod, mod, mod, g.reshape(1, D_MODEL), w1_bf16, w2_bf16)


def _final_norm_kernel(x_ref, g_ref, o_ref):
    g = g_ref[...]

    def body(c, carry):
        r0 = pl.multiple_of(c * ROW_CHUNK, ROW_CHUNK)
        x = x_ref[pl.ds(r0, ROW_CHUNK), :]
        ms = jnp.mean(x * x, axis=-1, keepdims=True)
        o_ref[pl.ds(r0, ROW_CHUNK), :] = x * lax.rsqrt(ms + EPS) * g
        return carry
    lax.fori_loop(0, TILE_M // ROW_CHUNK, body, 0)


def _final_norm(x, g):
    return pl.pallas_call(
        _final_norm_kernel,
        out_shape=jax.ShapeDtypeStruct((N_LAT, D_MODEL), _F32),
        grid=(N_LAT_TILES,),
        in_specs=[pl.BlockSpec((TILE_M, D_MODEL), lambda i: (i, 0)),
                  pl.BlockSpec((1, D_MODEL), lambda i: (0, 0))],
        out_specs=pl.BlockSpec((TILE_M, D_MODEL), lambda i: (i, 0)),
        compiler_params=pltpu.CompilerParams(
            dimension_semantics=("arbitrary",),
            vmem_limit_bytes=_vmem_limit(40 * 1024 * 1024)),
        name="final_norm",
    )(x, g.reshape(1, D_MODEL))


def kernel(x, c, ctx, c_ctx, w_ada, b_ada, g_norm1, w_in, rpb, conv_w, g_na_out, g_conv_out,
           w_out, g_norm2, w_mlp1, w_mlp2, g_final):
    tok = jnp.concatenate([x.reshape(N_LAT, D_MODEL), ctx.reshape(N_CTX, D_MODEL)], axis=0)
    cc = jnp.concatenate(
        [c, c_ctx[None, :], jnp.zeros((MOD_ROWS - BATCH - 1, D_MODEL), _F32)], axis=0)
    mod_all = _ada_modulation(cc, w_ada, b_ada)
    bias_all = _bias_tables(rpb)

    for i in range(DEPTH):
        last = i == DEPTH - 1
        mod = mod_all[i].reshape(MOD_ROWS, 1, N_MOD * D_MODEL)
        bias = bias_all[i * NA_HEADS:(i + 1) * NA_HEADS]
        p = _inproj(tok, mod, g_norm1[i], w_in[i].astype(_BF16))
        o_na = _attention(p, bias)
        with_ctx = not last
        if with_ctx:
            o_na = _ctx_attention(p, o_na)
        tok = _outproj(o_na, p, tok, mod, g_na_out[i], g_conv_out[i], conv_w[i],
                       w_out[i].astype(_BF16), with_ctx)
        tok = _mlp(tok, mod, g_norm2[i], w_mlp1[i].astype(_BF16), w_mlp2[i].astype(_BF16), with_ctx)

    out = _final_norm(tok, g_final)
    return out.reshape(BATCH, SEQ, D_MODEL)
```

```python
import functools

import numpy as np
import jax
import jax.numpy as jnp
from jax import lax
from jax.experimental import pallas as pl
from jax.experimental.pallas import tpu as pltpu

D_MODEL = 2048
BATCH = 4
SEQ = 8192
DEPTH = 4
CTX_LEN = 256
GRID_W = 64
GRID_ROWS = SEQ // GRID_W
NA_WIDTH = D_MODEL // 2
NA_HEADS = 8
HEAD_DIM = NA_WIDTH // NA_HEADS
CONV_WIDTH = D_MODEL - NA_WIDTH
CONV_K = 3
WIN_R = 8
WIN_C = 16
D_FF = 4 * D_MODEL
N_MOD = 6
IN_COLS = 3 * NA_WIDTH + 3 * CONV_WIDTH
EPS = 1e-6

N_LAT = BATCH * SEQ
N_CTX = BATCH * CTX_LEN
N_TOK = N_LAT + N_CTX

V7X_BF16_SUBLANES = 16
V7X_F32_SUBLANES = 8
V7X_VMEM_BYTES = 64 * 1024 * 1024

TILE_M_IN = 512
CHUNK_M = 256
TILE_M_MLP = 1024
ROW_CHUNK = 128
TILE_FF = 512
TILE_N_ADA = 1024
MOD_ROWS = 8

ATT_R = 4
ATT_KR = ATT_R + WIN_R - 1
ATT_MQ = ATT_R * GRID_W
ATT_NK = ATT_KR * GRID_W
ATT_NRB = GRID_ROWS // ATT_R
ATT_RB = 8
ATT_TYPES = 3
NEG_BIG = -1e30

_F32 = jnp.float32
_BF16 = jnp.bfloat16
_MIB = 1024 * 1024


def _vmem_limit(nbytes):
    return int(min(nbytes, V7X_VMEM_BYTES - 4 * _MIB))


def _mod_row(layer, i, tile_m):
    return layer * MOD_ROWS + jnp.minimum(i // (SEQ // tile_m), BATCH)


def _resident(block_shape, index_map):
    return pl.BlockSpec(block_shape, index_map, pipeline_mode=pl.Buffered(1))


def _token_specs(tile_m, split):
    if not split:
        return [pl.BlockSpec((tile_m, D_MODEL), lambda i, *_: (i, 0))]
    n_lat = N_LAT // tile_m
    return [pl.BlockSpec((tile_m, D_MODEL), lambda i, *_: (jnp.minimum(i, n_lat - 1), 0)),
            pl.BlockSpec((tile_m, D_MODEL), lambda i, *_: (jnp.maximum(i - n_lat, 0), 0),
                         pipeline_mode=pl.Buffered(1))]


def _with_token_source(i, tile_m, x_refs, fn):
    if len(x_refs) == 1:
        fn(x_refs[0])
        return
    n_lat = N_LAT // tile_m
    pl.when(i < n_lat)(lambda: fn(x_refs[0]))
    pl.when(i >= n_lat)(lambda: fn(x_refs[1]))


def _rms(x):
    return x * lax.rsqrt(jnp.mean(x * x, axis=-1, keepdims=True) + EPS)


def _ada_kernel(c_ref, w_ref, b_ref, o_ref):
    c = c_ref[...]
    s = c / (1.0 + jnp.exp(-c))
    o_ref[0] = jnp.dot(s.astype(_BF16), w_ref[0].astype(_BF16),
                       preferred_element_type=_F32) + b_ref[0]


def _ada_modulation(cc, w_ada, b_ada):
    n = N_MOD * D_MODEL
    return pl.pallas_call(
        _ada_kernel,
        out_shape=jax.ShapeDtypeStruct((DEPTH, MOD_ROWS, n), _F32),
        grid=(DEPTH, n // TILE_N_ADA),
        in_specs=[
            pl.BlockSpec((MOD_ROWS, D_MODEL), lambda i, j: (0, 0)),
            pl.BlockSpec((1, D_MODEL, TILE_N_ADA), lambda i, j: (i, 0, j)),
            pl.BlockSpec((1, 1, TILE_N_ADA), lambda i, j: (i, 0, j)),
        ],
        out_specs=pl.BlockSpec((1, MOD_ROWS, TILE_N_ADA), lambda i, j: (i, 0, j)),
        compiler_params=pltpu.CompilerParams(
            dimension_semantics=("arbitrary", "arbitrary"),
            vmem_limit_bytes=_vmem_limit(40 * _MIB)),
        name="ada_modulation",
    )(cc, w_ada, b_ada.reshape(DEPTH, 1, n))


def _bias_block_plan():
    plan = np.full((ATT_TYPES, ATT_R, ATT_KR), -1, np.int64)
    for t, rb in enumerate((0, 1, ATT_NRB - 1)):
        r0 = rb * ATT_R
        ks = int(np.clip(r0 - WIN_R // 2, 0, GRID_ROWS - ATT_KR))
        for ri in range(ATT_R):
            r = r0 + ri
            rs = int(np.clip(r - WIN_R // 2, 0, GRID_ROWS - WIN_R))
            for a in range(ATT_KR):
                kr = ks + a
                if rs <= kr < rs + WIN_R:
                    plan[t, ri, a] = kr - r + (WIN_R - 1)
    return plan


def _bias_kernel(rpb_ref, o_ref, m_ref, *, plan):
    ih = pl.program_id(0)
    n_dr, n_dc = 2 * WIN_R - 1, 2 * WIN_C - 1
    base = ih * (n_dr * n_dc)
    cq = lax.broadcasted_iota(jnp.int32, (GRID_W, GRID_W), 0)
    ck = lax.broadcasted_iota(jnp.int32, (GRID_W, GRID_W), 1)
    cs = jnp.clip(cq - WIN_C // 2, 0, GRID_W - WIN_C)
    in_win = (ck >= cs) & (ck < cs + WIN_C)
    dsel = jnp.where(in_win, ck - cq + (WIN_C - 1), -1)
    neg = jnp.full((GRID_W, GRID_W), NEG_BIG, _F32)
    for dr in range(n_dr):
        m = neg
        for d in range(n_dc):
            m = jnp.where(dsel == d, rpb_ref[base + dr * n_dc + d], m)
        m_ref[dr] = m
    for t in range(ATT_TYPES):
        for ri in range(ATT_R):
            for a in range(ATT_KR):
                dr = int(plan[t, ri, a])
                blk = m_ref[dr] if dr >= 0 else neg
                o_ref[0, t, ri * GRID_W:(ri + 1) * GRID_W, a * GRID_W:(a + 1) * GRID_W] = blk


def _bias_tables(rpb):
    n_dr, n_dc = 2 * WIN_R - 1, 2 * WIN_C - 1
    return pl.pallas_call(
        functools.partial(_bias_kernel, plan=_bias_block_plan()),
        out_shape=jax.ShapeDtypeStruct((DEPTH * NA_HEADS, ATT_TYPES, ATT_MQ, ATT_NK), _F32),
        grid=(DEPTH * NA_HEADS,),
        in_specs=[pl.BlockSpec(memory_space=pltpu.SMEM)],
        out_specs=pl.BlockSpec((1, ATT_TYPES, ATT_MQ, ATT_NK), lambda i: (i, 0, 0, 0)),
        scratch_shapes=[pltpu.VMEM((n_dr, GRID_W, GRID_W), _F32)],
        compiler_params=pltpu.CompilerParams(dimension_semantics=("arbitrary",)),
        name="attn_bias_tables",
    )(rpb.reshape(DEPTH * NA_HEADS * n_dr * n_dc))


def _inproj_kernel(*refs, split):
    nx = 2 if split else 1
    x_refs = refs[:nx]
    sh_ref, sc_ref, g_ref, w_ref, o_ref = refs[nx:]
    i = pl.program_id(0)

    def run(x_ref):
        gain = g_ref[...] * (1.0 + sc_ref[0])
        shift = sh_ref[0]
        for c in range(TILE_M_IN // CHUNK_M):
            rows = slice(c * CHUNK_M, (c + 1) * CHUNK_M)
            h = (_rms(x_ref[rows, :]) * gain + shift).astype(_BF16)
            q = jnp.dot(h, w_ref[:, 0:NA_WIDTH], preferred_element_type=_F32)
            o_ref[rows, 0:NA_WIDTH] = (q * HEAD_DIM ** -0.5).astype(_BF16)
            rest = jnp.dot(h, w_ref[:, NA_WIDTH:IN_COLS], preferred_element_type=_F32)
            o_ref[rows, NA_WIDTH:IN_COLS] = rest.astype(_BF16)

    _with_token_source(i, TILE_M_IN, x_refs, run)


def _inproj(layer, xs, mod, g, w_bf16):
    tm = TILE_M_IN
    split = len(xs) == 2
    return pl.pallas_call(
        functools.partial(_inproj_kernel, split=split),
        out_shape=jax.ShapeDtypeStruct((N_TOK, IN_COLS), _BF16),
        grid=(N_TOK // tm,),
        in_specs=_token_specs(tm, split) + [
            pl.BlockSpec((1, 1, D_MODEL), lambda i: (_mod_row(layer, i, tm), 0, 0)),
            pl.BlockSpec((1, 1, D_MODEL), lambda i: (_mod_row(layer, i, tm), 0, 1)),
            pl.BlockSpec((None, 1, D_MODEL), lambda i: (layer, 0, 0)),
            _resident((None, D_MODEL, IN_COLS), lambda i: (layer, 0, 0)),
        ],
        out_specs=pl.BlockSpec((tm, IN_COLS), lambda i: (i, 0)),
        compiler_params=pltpu.CompilerParams(
            dimension_semantics=("arbitrary",),
            vmem_limit_bytes=_vmem_limit(60 * _MIB)),
        name="inproj",
    )(*xs, mod, mod, g, w_bf16)


_NT_DIMS = (((1,), (1,)), ((), ()))


def _attn_kernel(q_ref, k_ref, v_ref, kc_ref, vc_ref, bias_ref, o_ref):
    g = pl.program_id(2)
    kc = kc_ref[...]
    vc = vc_ref[...]

    def body(t, carry):
        rb = g * ATT_RB + t
        ks = jnp.clip(rb * ATT_R - WIN_R // 2, 0, GRID_ROWS - ATT_KR)
        typ = jnp.where(rb == 0, 0, jnp.where(rb == ATT_NRB - 1, 2, 1))
        q0 = pl.multiple_of(t * ATT_MQ, ATT_MQ)
        k0 = pl.multiple_of(ks * GRID_W, GRID_W)
        q = q_ref[pl.ds(q0, ATT_MQ), :]
        kw = k_ref[pl.ds(k0, ATT_NK), :]
        vw = v_ref[pl.ds(k0, ATT_NK), :]
        s_loc = lax.dot_general(q, kw, _NT_DIMS, preferred_element_type=_F32) + bias_ref[0, typ]
        s_ctx = lax.dot_general(q, kc, _NT_DIMS, preferred_element_type=_F32)
        m = jnp.maximum(jnp.max(s_loc, axis=-1, keepdims=True),
                        jnp.max(s_ctx, axis=-1, keepdims=True))
        e_loc = jnp.exp(s_loc - m)
        e_ctx = jnp.exp(s_ctx - m)
        den = jnp.sum(e_loc, axis=-1, keepdims=True) + jnp.sum(e_ctx, axis=-1, keepdims=True)
        o = (jnp.dot(e_loc.astype(_BF16), vw, preferred_element_type=_F32)
             + jnp.dot(e_ctx.astype(_BF16), vc, preferred_element_type=_F32))
        o_ref[pl.ds(q0, ATT_MQ), :] = (o / den).astype(_BF16)
        return carry

    lax.fori_loop(0, ATT_RB, body, 0, unroll=2)


def _attention(layer, p, bias):
    steps = ATT_NRB // ATT_RB
    qrows = ATT_RB * ATT_MQ
    k_col = NA_WIDTH // HEAD_DIM
    ctx_blk0 = N_LAT // CTX_LEN
    return pl.pallas_call(
        _attn_kernel,
        out_shape=jax.ShapeDtypeStruct((N_TOK, NA_WIDTH), _BF16),
        grid=(BATCH, NA_HEADS, steps),
        in_specs=[
            pl.BlockSpec((qrows, HEAD_DIM), lambda b, h, g: (b * steps + g, h)),
            pl.BlockSpec((SEQ, HEAD_DIM), lambda b, h, g: (b, k_col + h)),
            pl.BlockSpec((SEQ, HEAD_DIM), lambda b, h, g: (b, 2 * k_col + h)),
            pl.BlockSpec((CTX_LEN, HEAD_DIM), lambda b, h, g: (ctx_blk0 + b, k_col + h)),
            pl.BlockSpec((CTX_LEN, HEAD_DIM), lambda b, h, g: (ctx_blk0 + b, 2 * k_col + h)),
            pl.BlockSpec((1, ATT_TYPES, ATT_MQ, ATT_NK),
                         lambda b, h, g: (layer * NA_HEADS + h, 0, 0, 0)),
        ],
        out_specs=pl.BlockSpec((qrows, HEAD_DIM), lambda b, h, g: (b * steps + g, h)),
        compiler_params=pltpu.CompilerParams(
            dimension_semantics=("arbitrary", "arbitrary", "arbitrary"),
            vmem_limit_bytes=_vmem_limit(40 * _MIB)),
        name="nbr_attention",
    )(p, p, p, p, p, bias)


def _ctx_attn_kernel(q_ref, k_ref, v_ref, o_in_ref, o_ref):
    del o_in_ref
    q = q_ref[...]
    s = lax.dot_general(q, k_ref[...], _NT_DIMS, preferred_element_type=_F32)
    m = jnp.max(s, axis=-1, keepdims=True)
    e = jnp.exp(s - m)
    den = jnp.sum(e, axis=-1, keepdims=True)
    o = jnp.dot(e.astype(_BF16), v_ref[...], preferred_element_type=_F32)
    o_ref[...] = (o / den).astype(_BF16)


def _ctx_attention(p, o_na):
    k_col = NA_WIDTH // HEAD_DIM
    ctx_blk0 = N_LAT // CTX_LEN
    return pl.pallas_call(
        _ctx_attn_kernel,
        out_shape=jax.ShapeDtypeStruct((N_TOK, NA_WIDTH), _BF16),
        grid=(BATCH, NA_HEADS),
        in_specs=[
            pl.BlockSpec((CTX_LEN, HEAD_DIM), lambda b, h: (ctx_blk0 + b, h)),
            pl.BlockSpec((CTX_LEN, HEAD_DIM), lambda b, h: (ctx_blk0 + b, k_col + h)),
            pl.BlockSpec((CTX_LEN, HEAD_DIM), lambda b, h: (ctx_blk0 + b, 2 * k_col + h)),
            pl.BlockSpec(memory_space=pl.ANY),
        ],
        out_specs=pl.BlockSpec((CTX_LEN, HEAD_DIM), lambda b, h: (ctx_blk0 + b, h)),
        input_output_aliases={3: 0},
        compiler_params=pltpu.CompilerParams(dimension_semantics=("arbitrary", "arbitrary")),
        name="ctx_attention",
    )(p, p, p, o_na)


_HALO = V7X_BF16_SUBLANES
_WPAD = V7X_F32_SUBLANES


def _outproj_kernel(*refs, split):
    (ona_ref, bg_ref, cg_ref, u_ref, cgp_ref, up_ref, cgn_ref, un_ref) = refs[:8]
    nx = 2 if split else 1
    x_refs = refs[8:8 + nx]
    gt_ref, gna_ref, gcv_ref, cw_ref, w_ref, o_ref, wbuf_ref = refs[8 + nx:]
    i = pl.program_id(0)
    tm = TILE_M_IN
    seq_len = jnp.where(i < N_LAT // tm, SEQ, CTX_LEN)

    wbuf_ref[_WPAD:_WPAD + tm, :] = cg_ref[...].astype(_F32) * u_ref[...].astype(_F32)
    wbuf_ref[_WPAD - 1:_WPAD, :] = (cgp_ref[_HALO - 1:_HALO, :].astype(_F32)
                                    * up_ref[_HALO - 1:_HALO, :].astype(_F32))
    wbuf_ref[_WPAD + tm:_WPAD + tm + 1, :] = (cgn_ref[0:1, :].astype(_F32)
                                              * un_ref[0:1, :].astype(_F32))

    def run(x_ref):
        cw = cw_ref[...]
        for c in range(tm // CHUNK_M):
            r0 = c * CHUNK_M
            rows = slice(r0, r0 + CHUNK_M)
            row = lax.broadcasted_iota(jnp.int32, (CHUNK_M, 1), 0) + (i * tm + r0)
            pos = row & (seq_len - 1)
            w_prev = jnp.where(pos == 0, 0.0, wbuf_ref[_WPAD - 1 + r0:_WPAD - 1 + r0 + CHUNK_M, :])
            w_mid = wbuf_ref[_WPAD + r0:_WPAD + r0 + CHUNK_M, :]
            w_next = jnp.where(pos == seq_len - 1, 0.0,
                               wbuf_ref[_WPAD + 1 + r0:_WPAD + 1 + r0 + CHUNK_M, :])
            conv = w_prev * cw[0:1, :] + w_mid * cw[1:2, :] + w_next * cw[2:3, :]
            oc_n = _rms(bg_ref[rows, :].astype(_F32) * conv) * gcv_ref[...]
            on_n = _rms(ona_ref[rows, :].astype(_F32)) * gna_ref[...]
            y = (jnp.dot(on_n.astype(_BF16), w_ref[0:NA_WIDTH, :], preferred_element_type=_F32)
                 + jnp.dot(oc_n.astype(_BF16), w_ref[NA_WIDTH:D_MODEL, :],
                           preferred_element_type=_F32))
            o_ref[rows, :] = x_ref[rows, :] + gt_ref[0] * y

    _with_token_source(i, tm, x_refs, run)


def _outproj(layer, o_na, p, xs, mod, g_na, g_conv, conv_w, w_bf16, with_ctx):
    tm = TILE_M_IN
    split = len(xs) == 2
    n_tiles = (N_TOK if with_ctx else N_LAT) // tm
    cb = NA_WIDTH // CONV_WIDTH * 3
    halo_per_tile = tm // _HALO
    last_halo = N_TOK // _HALO - 1
    prev_map = lambda col: (lambda i: (jnp.maximum(i * halo_per_tile - 1, 0), col))
    next_map = lambda col: (lambda i: (jnp.minimum((i + 1) * halo_per_tile, last_halo), col))
    n_in = 8 + len(xs)
    return pl.pallas_call(
        functools.partial(_outproj_kernel, split=split),
        out_shape=jax.ShapeDtypeStruct((N_TOK, D_MODEL), _F32),
        grid=(n_tiles,),
        in_specs=[
            pl.BlockSpec((tm, NA_WIDTH), lambda i: (i, 0)),
            pl.BlockSpec((tm, CONV_WIDTH), lambda i: (i, cb)),
            pl.BlockSpec((tm, CONV_WIDTH), lambda i: (i, cb + 1)),
            pl.BlockSpec((tm, CONV_WIDTH), lambda i: (i, cb + 2)),
            pl.BlockSpec((_HALO, CONV_WIDTH), prev_map(cb + 1)),
            pl.BlockSpec((_HALO, CONV_WIDTH), prev_map(cb + 2)),
            pl.BlockSpec((_HALO, CONV_WIDTH), next_map(cb + 1)),
            pl.BlockSpec((_HALO, CONV_WIDTH), next_map(cb + 2)),
        ] + _token_specs(tm, split) + [
            pl.BlockSpec((1, 1, D_MODEL), lambda i: (_mod_row(layer, i, tm), 0, 2)),
            pl.BlockSpec((None, 1, NA_WIDTH), lambda i: (layer, 0, 0)),
            pl.BlockSpec((None, 1, CONV_WIDTH), lambda i: (layer, 0, 0)),
            pl.BlockSpec((None, CONV_K, CONV_WIDTH), lambda i: (layer, 0, 0)),
            _resident((None, D_MODEL, D_MODEL), lambda i: (layer, 0, 0)),
        ],
        out_specs=pl.BlockSpec((tm, D_MODEL), lambda i: (i, 0)),
        scratch_shapes=[pltpu.VMEM((tm + 2 * _WPAD, CONV_WIDTH), _F32)],
        input_output_aliases={} if split else {8: 0},
        compiler_params=pltpu.CompilerParams(
            dimension_semantics=("arbitrary",),
            vmem_limit_bytes=_vmem_limit(60 * _MIB)),
        name="outproj",
    )(o_na, p, p, p, p, p, p, p, *xs, mod, g_na, g_conv, conv_w, w_bf16)


def _mlp_kernel(x_ref, sh_ref, sc_ref, gt_ref, g_ref, w1_ref, w2_ref, *rest, final):
    if final:
        gf_ref, o_ref, h_ref = rest
    else:
        o_ref, h_ref = rest
    f = pl.program_id(1)

    @pl.when(f == 0)
    def _():
        gain = g_ref[...] * (1.0 + sc_ref[0])
        shift = sh_ref[0]

        def body(c, carry):
            r0 = pl.multiple_of(c * ROW_CHUNK, ROW_CHUNK)
            h = _rms(x_ref[pl.ds(r0, ROW_CHUNK), :]) * gain + shift
            h_ref[pl.ds(r0, ROW_CHUNK), :] = h.astype(_BF16)
            return carry
        lax.fori_loop(0, TILE_M_MLP // ROW_CHUNK, body, 0)
        o_ref[...] = jnp.zeros_like(o_ref)

    a = jnp.dot(h_ref[...], w1_ref[...], preferred_element_type=_F32)
    a = jnp.maximum(a, 0.0)
    a = a * a
    o_ref[...] += jnp.dot(a.astype(_BF16), w2_ref[...], preferred_element_type=_F32)

    @pl.when(f == pl.num_programs(1) - 1)
    def _():
        if not final:
            o_ref[...] = x_ref[...] + gt_ref[0] * o_ref[...]
        else:
            def body(c, carry):
                r0 = pl.multiple_of(c * ROW_CHUNK, ROW_CHUNK)
                rows = pl.ds(r0, ROW_CHUNK)
                o_ref[rows, :] = _rms(x_ref[rows, :] + gt_ref[0] * o_ref[rows, :]) * gf_ref[...]
                return carry
            lax.fori_loop(0, TILE_M_MLP // ROW_CHUNK, body, 0)


def _mlp(layer, x, mod, g, w1_bf16, w2_bf16, with_ctx, g_final=None):
    tm = TILE_M_MLP
    final = g_final is not None
    n_rows = N_TOK if with_ctx else N_LAT
    in_specs = [
        pl.BlockSpec((tm, D_MODEL), lambda i, f: (i, 0)),
        pl.BlockSpec((1, 1, D_MODEL), lambda i, f: (_mod_row(layer, i, tm), 0, 3)),
        pl.BlockSpec((1, 1, D_MODEL), lambda i, f: (_mod_row(layer, i, tm), 0, 4)),
        pl.BlockSpec((1, 1, D_MODEL), lambda i, f: (_mod_row(layer, i, tm), 0, 5)),
        pl.BlockSpec((None, 1, D_MODEL), lambda i, f: (layer, 0, 0)),
        pl.BlockSpec((None, D_MODEL, TILE_FF), lambda i, f: (layer, 0, f)),
        pl.BlockSpec((None, TILE_FF, D_MODEL), lambda i, f: (layer, f, 0)),
    ]
    args = [x, mod, mod, mod, g, w1_bf16, w2_bf16]
    if final:
        in_specs.append(pl.BlockSpec((1, D_MODEL), lambda i, f: (0, 0)))
        args.append(g_final.reshape(1, D_MODEL))
    return pl.pallas_call(
        functools.partial(_mlp_kernel, final=final),
        out_shape=jax.ShapeDtypeStruct((N_LAT if final else N_TOK, D_MODEL), _F32),
        grid=(n_rows // tm, D_FF // TILE_FF),
        in_specs=in_specs,
        out_specs=pl.BlockSpec((tm, D_MODEL), lambda i, f: (i, 0)),
        scratch_shapes=[pltpu.VMEM((tm, D_MODEL), _BF16)],
        input_output_aliases={} if final else {0: 0},
        compiler_params=pltpu.CompilerParams(
            dimension_semantics=("arbitrary", "arbitrary"),
            vmem_limit_bytes=_vmem_limit(60 * _MIB)),
        name="mlp",
    )(*args)


def kernel(x, c, ctx, c_ctx, w_ada, b_ada, g_norm1, w_in, rpb, conv_w, g_na_out, g_conv_out,
           w_out, g_norm2, w_mlp1, w_mlp2, g_final):
    cc = jnp.concatenate(
        [c, c_ctx[None, :], jnp.zeros((MOD_ROWS - BATCH - 1, D_MODEL), _F32)], axis=0)
    mod = _ada_modulation(cc, w_ada, b_ada).reshape(DEPTH * MOD_ROWS, 1, N_MOD * D_MODEL)
    bias = _bias_tables(rpb)
    w_in_b, w_out_b = w_in.astype(_BF16), w_out.astype(_BF16)
    w1_b, w2_b = w_mlp1.astype(_BF16), w_mlp2.astype(_BF16)
    g1 = g_norm1.reshape(DEPTH, 1, D_MODEL)
    g2 = g_norm2.reshape(DEPTH, 1, D_MODEL)
    g_na = g_na_out.reshape(DEPTH, 1, NA_WIDTH)
    g_cv = g_conv_out.reshape(DEPTH, 1, CONV_WIDTH)

    xs = (x.reshape(N_LAT, D_MODEL), ctx.reshape(N_CTX, D_MODEL))
    for layer in range(DEPTH):
        last = layer == DEPTH - 1
        with_ctx = not last
        p = _inproj(layer, xs, mod, g1, w_in_b)
        o_na = _attention(layer, p, bias)
        if with_ctx:
            o_na = _ctx_attention(p, o_na)
        tok = _outproj(layer, o_na, p, xs, mod, g_na, g_cv, conv_w, w_out_b, with_ctx)
        tok = _mlp(layer, tok, mod, g2, w1_b, w2_b, with_ctx, g_final if last else None)
        xs = (tok,)
    return tok.reshape(BATCH, SEQ, D_MODEL)
```
